```python
import jax, jax.numpy as jnp
from jax import lax
import numpy as np

D_MODEL = 2048
BATCH = 2
SEQ = 4096
DEPTH = 4
DEC_BATCH = 128
DEC_SEQ = 1
PAST_LEN = 8192
PAGE_SIZE = 128

N_EVEN = (DEPTH + 1) // 2
N_ODD = DEPTH // 2
RET_HEADS = 8
RET_DK = 256
RET_DV = 256
CHUNK_RET = 128
MLA_HEADS = 16
MLA_Q_RANK = 512
MLA_KV_RANK = 512
MLA_NOPE = 128
MLA_ROPE = 64
MLA_VD = 128
MLA_SCALE = (MLA_NOPE + MLA_ROPE) ** -0.5
Q_BLOCK = 128
GLA_HEADS = 4
GLA_DK = 256
GLA_DV = 512
GLA_GATE_RANK = 16
GLA_TAU = 16.0
CHUNK_GLA = 64
MEM_LEN = 256
MEM_HEADS = 4
MEM_HD = 128
ROPE_BASE = 10000.0
RMS_EPS = 1e-6
LN_EPS = 1e-5
DN_ALPHA = (2.0 * DEPTH) ** 0.25
DN_BETA = (8.0 * DEPTH) ** -0.25

RET_QK_W = RET_HEADS * RET_DK
RET_V_W = RET_HEADS * RET_DV
MLA_V_W = MLA_HEADS * MLA_VD
GLA_K_W = GLA_HEADS * GLA_DK
GLA_V_W = GLA_HEADS * GLA_DV
MEM_W = MEM_HEADS * MEM_HD
EVEN_SPLITS = (RET_QK_W, RET_QK_W, RET_V_W, RET_V_W, MLA_Q_RANK, MLA_KV_RANK, MLA_ROPE, MLA_V_W, MEM_W, MEM_W)
EVEN_IN = sum(EVEN_SPLITS)
EVEN_OUT = RET_V_W + MLA_V_W + MEM_W
ODD_SPLITS = (GLA_K_W, GLA_K_W, GLA_V_W, GLA_V_W, GLA_GATE_RANK, MEM_W, MEM_W)
ODD_IN = sum(ODD_SPLITS)
ODD_OUT = GLA_V_W + MEM_W

kernel_name = 'hybrid_retention_mla_gla_memory_step'


def split_cols(h, sizes):
    cuts = [int(c) for c in np.cumsum(sizes)[:-1]]
    return jnp.split(h, cuts, axis=-1)


def rms_norm(x, g=None):
    xf = x.astype(jnp.float32)
    y = xf * lax.rsqrt(jnp.mean(xf * xf, axis=-1, keepdims=True) + RMS_EPS)
    if g is not None:
        y = y * g.astype(jnp.float32)
    return y.astype(x.dtype)


def layer_norm(x, g, b):
    xf = x.astype(jnp.float32)
    mu = jnp.mean(xf, axis=-1, keepdims=True)
    xc = xf - mu
    var = jnp.mean(xc * xc, axis=-1, keepdims=True)
    return (xc * lax.rsqrt(var + LN_EPS) * g.astype(jnp.float32) + b.astype(jnp.float32)).astype(x.dtype)


def rope(x, pos):
    half = x.shape[-1] // 2
    inv = ROPE_BASE ** (-jnp.arange(half, dtype=jnp.float32) / half)
    ang = pos.astype(jnp.float32)[:, None] * inv[None, :]
    cos = jnp.cos(ang)[None, :, None, :]
    sin = jnp.sin(ang)[None, :, None, :]
    x1 = x[..., :half].astype(jnp.float32)
    x2 = x[..., half:].astype(jnp.float32)
    return jnp.concatenate([x1 * cos - x2 * sin, x1 * sin + x2 * cos], axis=-1).astype(x.dtype)


def chunked_gated_linear_attention(q, k, v, log_g, s0, chunk):
    B, T, H, _ = q.shape
    dv = v.shape[-1]
    c = min(chunk, T)
    n = -(-T // c)
    pad = n * c - T
    if pad:
        widths = ((0, 0), (0, pad), (0, 0), (0, 0))
        q, k, v, log_g = (jnp.pad(a, widths) for a in (q, k, v, log_g))

    def to_chunks(a):
        return a.reshape(B, n, c, H, a.shape[-1]).transpose(1, 0, 3, 2, 4).astype(jnp.float32)

    scalar_decay = log_g.shape[-1] == 1
    mask = jnp.tril(jnp.ones((c, c), dtype=bool))[:, :, None]

    def step(S, inp):
        qi, ki, vi, gi = inp
        b = jnp.cumsum(gi, axis=2)
        b_last = b[:, :, -1:, :]
        inter = jnp.einsum('bhjd,bhde->bhje', qi * jnp.exp(b), S)
        rel = b[:, :, :, None, :] - b[:, :, None, :, :]
        decay = jnp.exp(jnp.where(mask, rel, -jnp.inf))
        if scalar_decay:
            scores = jnp.einsum('bhjd,bhld->bhjl', qi, ki) * decay[..., 0]
        else:
            scores = jnp.einsum('bhjd,bhld,bhjld->bhjl', qi, ki, decay)
        out = inter + jnp.einsum('bhjl,bhle->bhje', scores, vi)
        S_new = jnp.swapaxes(jnp.exp(b_last), -1, -2) * S + jnp.einsum('bhld,bhle->bhde', ki * jnp.exp(b_last - b), vi)
        return S_new, out

    S, outs = lax.scan(step, s0.astype(jnp.float32), (to_chunks(q), to_chunks(k), to_chunks(v), to_chunks(log_g)))
    o = outs.transpose(1, 0, 3, 2, 4).reshape(B, n * c, H, dv)[:, :T]
    return o.astype(v.dtype), S


def memory_attention(q, mk, mv):
    s = jnp.einsum('bthd,bmhd->bhtm', q, mk).astype(jnp.float32) * (MEM_HD ** -0.5)
    p = jax.nn.softmax(s, axis=-1).astype(mv.dtype)
    return jnp.einsum('bhtm,bmhd->bthd', p, mv)


def mla_prompt_attention(q_nope, q_pe, ckv, kpe, w_uk, w_uv):
    B, T, H, _ = q_nope.shape
    k_nope = jnp.einsum('btr,rhn->bthn', ckv, w_uk)
    v = jnp.einsum('btr,rhv->bthv', ckv, w_uv)
    qb = min(Q_BLOCK, T)
    nb = T // qb

    def blocks(a):
        return a.reshape((B, nb, qb) + a.shape[2:]).swapaxes(0, 1)

    starts = jnp.arange(nb) * qb
    kpos = jnp.arange(T)

    def attend(args):
        qn, qp, start = args
        s = (jnp.einsum('bqhn,bkhn->bhqk', qn, k_nope) + jnp.einsum('bqhp,bkp->bhqk', qp, kpe)).astype(jnp.float32) * MLA_SCALE
        qpos = start + jnp.arange(qb)
        s = jnp.where(kpos[None, :] <= qpos[:, None], s, -jnp.inf)
        p = jax.nn.softmax(s, axis=-1).astype(v.dtype)
        return jnp.einsum('bhqk,bkhv->bqhv', p, v)

    o = lax.map(attend, (blocks(q_nope), blocks(q_pe), starts))
    return o.swapaxes(0, 1).reshape(B, T, H, MLA_VD)


def mla_cached_attention(q_nope, q_pe, ckv, kpe, past_ckv, past_kpe, w_uk, w_uv):
    T = q_nope.shape[1]
    L = past_ckv.shape[1]
    q_lat = jnp.einsum('bthn,rhn->bthr', q_nope, w_uk)
    s_past = jnp.einsum('bthr,blr->bhtl', q_lat, past_ckv) + jnp.einsum('bthp,blp->bhtl', q_pe, past_kpe)
    s_new = jnp.einsum('bthr,bsr->bhts', q_lat, ckv) + jnp.einsum('bthp,bsp->bhts', q_pe, kpe)
    s = jnp.concatenate([s_past, s_new], axis=-1).astype(jnp.float32) * MLA_SCALE
    keep = jnp.concatenate([jnp.ones((T, L), dtype=bool), jnp.tril(jnp.ones((T, T), dtype=bool))], axis=-1)
    p = jax.nn.softmax(jnp.where(keep, s, -jnp.inf), axis=-1).astype(ckv.dtype)
    o_lat = jnp.einsum('bhtl,blr->bthr', p[..., :L], past_ckv) + jnp.einsum('bhts,bsr->bthr', p[..., L:], ckv)
    return jnp.einsum('bthr,rhv->bthv', o_lat, w_uv)


def even_mixer(x, pos, w_in, w_uq, w_uk, w_uv, q_norm_g, kv_norm_g, w_out, ret_state, mla_past, mem_k, mem_v):
    B, T, _ = x.shape
    h = x @ w_in
    rq, rk, rv, rg, cq, ckv, kpe, mgate, xq, xg = split_cols(h, EVEN_SPLITS)
    rq = rope(rq.reshape(B, T, RET_HEADS, RET_DK), pos)
    rk = rope(rk.reshape(B, T, RET_HEADS, RET_DK), pos) * (RET_DK ** -0.5)
    rv = rv.reshape(B, T, RET_HEADS, RET_DV)
    log_gamma = jnp.log1p(-jnp.exp2(-5.0 - jnp.arange(RET_HEADS, dtype=jnp.float32)))
    lg = jnp.broadcast_to(log_gamma[None, None, :, None], (B, T, RET_HEADS, 1))
    ro, ret_new = chunked_gated_linear_attention(rq, rk, rv, lg, ret_state, CHUNK_RET)
    ro = rms_norm(ro).reshape(B, T, RET_V_W) * jax.nn.silu(rg)
    qf = (rms_norm(cq, q_norm_g) @ w_uq).reshape(B, T, MLA_HEADS, MLA_NOPE + MLA_ROPE)
    q_nope = qf[..., :MLA_NOPE]
    q_pe = rope(qf[..., MLA_NOPE:], pos)
    ckv = rms_norm(ckv, kv_norm_g)
    kpe = rope(kpe[:, :, None, :], pos)[:, :, 0, :]
    if mla_past is None:
        mo = mla_prompt_attention(q_nope, q_pe, ckv, kpe, w_uk, w_uv)
    else:
        mo = mla_cached_attention(q_nope, q_pe, ckv, kpe, mla_past[0], mla_past[1], w_uk, w_uv)
    mo = mo.reshape(B, T, MLA_V_W) * jax.nn.silu(mgate)
    xo = memory_attention(xq.reshape(B, T, MEM_HEADS, MEM_HD), mem_k, mem_v).reshape(B, T, MEM_W) * jax.nn.silu(xg)
    y = jnp.concatenate([ro, mo, xo], axis=-1) @ w_out
    return y, ret_new, ckv, kpe


def odd_mixer(x, w_in, w_gk2, b_gk, gla_norm_g, w_out, gla_state, mem_k, mem_v):
    B, T, _ = x.shape
    h = x @ w_in
    gq, gk, gv, gg, glr, xq, xg = split_cols(h, ODD_SPLITS)
    gq = gq.reshape(B, T, GLA_HEADS, GLA_DK) * (GLA_DK ** -0.5)
    gk = gk.reshape(B, T, GLA_HEADS, GLA_DK)
    gv = gv.reshape(B, T, GLA_HEADS, GLA_DV)
    log_a = jax.nn.log_sigmoid((glr @ w_gk2 + b_gk).astype(jnp.float32)) / GLA_TAU
    log_a = log_a.reshape(B, T, GLA_HEADS, GLA_DK)
    go, gla_new = chunked_gated_linear_attention(gq, gk, gv, log_a, gla_state, CHUNK_GLA)
    go = rms_norm(go, gla_norm_g).reshape(B, T, GLA_V_W) * jax.nn.silu(gg)
    xo = memory_attention(xq.reshape(B, T, MEM_HEADS, MEM_HD), mem_k, mem_v).reshape(B, T, MEM_W) * jax.nn.silu(xg)
    y = jnp.concatenate([go, xo], axis=-1) @ w_out
    return y, gla_new


def setup_inputs(seed: int = 0) -> dict:
    key = jax.random.key(seed)
    ks = iter(jax.random.split(key, 40))
    f32 = jnp.float32

    def nrm(shape, scale=1.0):
        return jax.random.normal(next(ks), shape, f32) * scale

    n_pages = PAST_LEN // PAGE_SIZE
    used = DEC_BATCH * n_pages
    pool = used + max(1, used // 4)
    page_table = jax.random.permutation(next(ks), pool)[:used].reshape(DEC_BATCH, n_pages).astype(jnp.int32)
    return {
        'x_prompt': nrm((BATCH, SEQ, D_MODEL)),
        'x_sample': nrm((DEC_BATCH, DEC_SEQ, D_MODEL)),
        'mem_prompt': nrm((BATCH, MEM_LEN, D_MODEL)),
        'cache_mla_ckv': nrm((N_EVEN, pool, PAGE_SIZE, MLA_KV_RANK)),
        'cache_mla_kpe': nrm((N_EVEN, pool, PAGE_SIZE, MLA_ROPE)),
        'page_table': page_table,
        'state_ret': nrm((N_EVEN, DEC_BATCH, RET_HEADS, RET_DK, RET_DV)),
        'state_gla': nrm((N_ODD, DEC_BATCH, GLA_HEADS, GLA_DK, GLA_DV)),
        'cache_mem_k': nrm((DEPTH, DEC_BATCH, MEM_LEN, MEM_HEADS, MEM_HD)),
        'cache_mem_v': nrm((DEPTH, DEC_BATCH, MEM_LEN, MEM_HEADS, MEM_HD)),
        'w_in_even': nrm((N_EVEN, D_MODEL, EVEN_IN), D_MODEL ** -0.5),
        'w_uq': nrm((N_EVEN, MLA_Q_RANK, MLA_HEADS * (MLA_NOPE + MLA_ROPE)), MLA_Q_RANK ** -0.5),
        'w_uk': nrm((N_EVEN, MLA_KV_RANK, MLA_HEADS, MLA_NOPE), MLA_KV_RANK ** -0.5),
        'w_uv': nrm((N_EVEN, MLA_KV_RANK, MLA_HEADS, MLA_VD), MLA_KV_RANK ** -0.5),
        'q_norm_g': 1.0 + nrm((N_EVEN, MLA_Q_RANK), 0.1),
        'kv_norm_g': 1.0 + nrm((N_EVEN, MLA_KV_RANK), 0.1),
        'w_out_even': nrm((N_EVEN, EVEN_OUT, D_MODEL), EVEN_OUT ** -0.5 * DN_BETA),
        'w_in_odd': nrm((N_ODD, D_MODEL, ODD_IN), D_MODEL ** -0.5),
        'w_gk2': nrm((N_ODD, GLA_GATE_RANK, GLA_K_W), GLA_GATE_RANK ** -0.5),
        'b_gk': nrm((N_ODD, GLA_K_W), 0.1),
        'gla_norm_g': 1.0 + nrm((N_ODD, GLA_DV), 0.1),
        'w_out_odd': nrm((N_ODD, ODD_OUT, D_MODEL), ODD_OUT ** -0.5 * DN_BETA),
        'w_mem_k': nrm((DEPTH, D_MODEL, MEM_W), D_MODEL ** -0.5),
        'w_mem_v': nrm((DEPTH, D_MODEL, MEM_W), D_MODEL ** -0.5),
        'ln_g': 1.0 + nrm((DEPTH, D_MODEL), 0.1),
        'ln_b': nrm((DEPTH, D_MODEL), 0.02),
    }


def reference(x_prompt, x_sample, mem_prompt, cache_mla_ckv, cache_mla_kpe, page_table, state_ret, state_gla,
              cache_mem_k, cache_mem_v, w_in_even, w_uq, w_uk, w_uv, q_norm_g, kv_norm_g, w_out_even,
              w_in_odd, w_gk2, b_gk, gla_norm_g, w_out_odd, w_mem_k, w_mem_v, ln_g, ln_b):
    B, T = x_prompt.shape[0], x_prompt.shape[1]
    DB, TS = x_sample.shape[0], x_sample.shape[1]
    past_len = page_table.shape[1] * PAGE_SIZE
    pos_p = jnp.arange(T)
    pos_s = past_len + jnp.arange(TS)
    xp, xs = x_prompt, x_sample
    ret_p, ret_s, gla_p, gla_s = [], [], [], []
    ckv_p, kpe_p, ckv_s, kpe_s = [], [], [], []
    memk_p, memv_p = [], []
    for layer in range(DEPTH):
        mk_p = (mem_prompt @ w_mem_k[layer]).reshape(B, MEM_LEN, MEM_HEADS, MEM_HD)
        mv_p = (mem_prompt @ w_mem_v[layer]).reshape(B, MEM_LEN, MEM_HEADS, MEM_HD)
        memk_p.append(mk_p)
        memv_p.append(mv_p)
        if layer % 2 == 0:
            e = layer // 2
            weights = (w_in_even[e], w_uq[e], w_uk[e], w_uv[e], q_norm_g[e], kv_norm_g[e], w_out_even[e])
            zero_ret = jnp.zeros((B, RET_HEADS, RET_DK, RET_DV), jnp.float32)
            yp, sp, cp, kp = even_mixer(xp, pos_p, *weights, zero_ret, None, mk_p, mv_p)
            past = (cache_mla_ckv[e, page_table].reshape(DB, past_len, MLA_KV_RANK),
                    cache_mla_kpe[e, page_table].reshape(DB, past_len, MLA_ROPE))
            ys, ss, cs, kss = even_mixer(xs, pos_s, *weights, state_ret[e], past, cache_mem_k[layer], cache_mem_v[layer])
            ret_p.append(sp)
            ret_s.append(ss)
            ckv_p.append(cp)
            kpe_p.append(kp)
            ckv_s.append(cs)
            kpe_s.append(kss)
        else:
            o = layer // 2
            weights = (w_in_odd[o], w_gk2[o], b_gk[o], gla_norm_g[o], w_out_odd[o])
            zero_gla = jnp.zeros((B, GLA_HEADS, GLA_DK, GLA_DV), jnp.float32)
            yp, sp = odd_mixer(xp, *weights, zero_gla, mk_p, mv_p)
            ys, ss = odd_mixer(xs, *weights, state_gla[o], cache_mem_k[layer], cache_mem_v[layer])
            gla_p.append(sp)
            gla_s.append(ss)
        xp = layer_norm(DN_ALPHA * xp + yp, ln_g[layer], ln_b[layer])
        xs = layer_norm(DN_ALPHA * xs + ys, ln_g[layer], ln_b[layer])
    return (xp, xs, jnp.stack(ret_p), jnp.stack(ret_s), jnp.stack(gla_p), jnp.stack(gla_s),
            jnp.stack(ckv_p), jnp.stack(kpe_p), jnp.stack(ckv_s), jnp.stack(kpe_s),
            jnp.stack(memk_p), jnp.stack(memv_p))
```

```python
import functools
import math

import jax
import jax.numpy as jnp
import numpy as np
from jax import lax
from jax.experimental import pallas as pl
from jax.experimental.pallas import tpu as pltpu

F32 = jnp.float32
BF16 = jnp.bfloat16

D_MODEL = 2048
DEPTH = 4
PAGE_SIZE = 128
RET_HEADS = 8
RET_DK = 256
RET_DV = 256
CHUNK_RET = 128
MLA_HEADS = 16
MLA_Q_RANK = 512
MLA_KV_RANK = 512
MLA_NOPE = 128
MLA_ROPE = 64
MLA_VD = 128
MLA_SCALE = (MLA_NOPE + MLA_ROPE) ** -0.5
GLA_HEADS = 4
GLA_DK = 256
GLA_DV = 512
GLA_GATE_RANK = 16
GLA_TAU = 16.0
CHUNK_GLA = 64
MEM_LEN = 256
MEM_HEADS = 4
MEM_HD = 128
ROPE_BASE = 10000.0
RMS_EPS = 1e-6
LN_EPS = 1e-5
DN_ALPHA = (2.0 * DEPTH) ** 0.25

RET_W = RET_HEADS * RET_DK
MLA_V_W = MLA_HEADS * MLA_VD
GLA_K_W = GLA_HEADS * GLA_DK
GLA_V_W = GLA_HEADS * GLA_DV
MEM_W = MEM_HEADS * MEM_HD
LANE = 128
MLA_QK_PAD = 2 * LANE

E_RQ, E_RK, E_RV, E_RG = 0, 2048, 4096, 6144
E_CQ, E_CKV, E_MG, E_XQ, E_XG = 8192, 8704, 9216, 11264, 11776
E_MAIN = 12288
O_GQ, O_GK, O_GV, O_GG, O_XQ, O_XG = 0, 1024, 2048, 4096, 6144, 6656
O_MAIN = 7168

VMEM_LIMIT = 48 * 1024 * 1024


def _cparams(*sem):
    return pltpu.CompilerParams(dimension_semantics=sem, vmem_limit_bytes=VMEM_LIMIT)


def _silu(x):
    return x / (1.0 + jnp.exp(-x))


def _dot(a, b):
    return jnp.dot(a.astype(BF16), b.astype(BF16), preferred_element_type=F32)


def _dot_nt(a, b):
    return lax.dot_general(a.astype(BF16), b.astype(BF16), (((1,), (1,)), ((), ())),
                           preferred_element_type=F32)


def _dot_tn(a, b):
    return lax.dot_general(a.astype(BF16), b.astype(BF16), (((0,), (0,)), ((), ())),
                           preferred_element_type=F32)


def _rms(x):
    return x * lax.rsqrt(jnp.mean(x * x, axis=-1, keepdims=True) + RMS_EPS)


def _rope_half(x, cos, sin):
    x1 = x[:, :LANE]
    x2 = x[:, LANE:]
    return jnp.concatenate([x1 * cos - x2 * sin, x1 * sin + x2 * cos], axis=-1)


def _rope_small(x, cos_t, sin_a, sin_b):
    return x * cos_t + pltpu.roll(x, 96, 1) * sin_a + pltpu.roll(x, 32, 1) * sin_b


def _row_to_col(row):
    n = row.shape[1]
    eye = lax.broadcasted_iota(jnp.int32, (n, n), 0) == lax.broadcasted_iota(jnp.int32, (n, n), 1)
    return jnp.sum(jnp.where(eye, jnp.broadcast_to(row, (n, n)), 0.0), axis=1, keepdims=True)


def _mm_body(x_ref, w_ref, o_ref):
    o_ref[...] = _dot(x_ref[...], w_ref[...]).astype(o_ref.dtype)


def _matmul(x, w, tm, tn, out_dtype=F32):
    m, k = x.shape
    n = w.shape[1]
    tm = min(tm, m)
    tn = min(tn, n)
    return pl.pallas_call(
        _mm_body,
        grid=(m // tm, n // tn),
        in_specs=[pl.BlockSpec((tm, k), lambda i, j: (i, 0)),
                  pl.BlockSpec((k, tn), lambda i, j: (0, j))],
        out_specs=pl.BlockSpec((tm, tn), lambda i, j: (i, j)),
        out_shape=jax.ShapeDtypeStruct((m, n), out_dtype),
        compiler_params=_cparams("parallel", "parallel"),
        name="matmul",
    )(x, w)


def _kpe_body(x_ref, w_ref, c_ref, sa_ref, sb_ref, o_ref):
    acc = _dot(x_ref[...], w_ref[...])
    o_ref[...] = _rope_small(acc, c_ref[...], sa_ref[...], sb_ref[...])


def _kpe_proj(x, w, tabs, nb, t, tm):
    k = x.shape[1]
    tm = min(tm, t)
    nt = t // tm
    tab = pl.BlockSpec((tm, LANE), lambda b, i: (i, 0))
    return pl.pallas_call(
        _kpe_body,
        grid=(nb, nt),
        in_specs=[pl.BlockSpec((tm, k), lambda b, i: (b * nt + i, 0)),
                  pl.BlockSpec((k, LANE), lambda b, i: (0, 0)), tab, tab, tab],
        out_specs=pl.BlockSpec((tm, LANE), lambda b, i: (b * nt + i, 0)),
        out_shape=jax.ShapeDtypeStruct((nb * t, LANE), F32),
        compiler_params=_cparams("parallel", "parallel"),
        name="kpe_proj",
    )(x, w, *tabs)


def _q_body(x_ref, g_ref, w_ref, c_ref, sa_ref, sb_ref, o_ref):
    xn = _rms(x_ref[...]) * g_ref[...]
    acc = _dot(xn, w_ref[...])
    o_ref[:, :LANE] = acc[:, :LANE].astype(o_ref.dtype)
    o_ref[:, LANE:] = _rope_small(acc[:, LANE:], c_ref[...], sa_ref[...], sb_ref[...]).astype(o_ref.dtype)


def _q_proj(h, col_blk, g, w, tabs, nb, t, tm):
    tm = min(tm, t)
    nt = t // tm
    tab = pl.BlockSpec((tm, LANE), lambda b, i, hh: (i, 0))
    return pl.pallas_call(
        _q_body,
        grid=(nb, nt, MLA_HEADS),
        in_specs=[pl.BlockSpec((tm, MLA_Q_RANK), lambda b, i, hh: (b * nt + i, col_blk)),
                  pl.BlockSpec((1, MLA_Q_RANK), lambda b, i, hh: (0, 0)),
                  pl.BlockSpec((MLA_Q_RANK, MLA_QK_PAD), lambda b, i, hh: (0, hh)),
                  tab, tab, tab],
        out_specs=pl.BlockSpec((tm, MLA_QK_PAD), lambda b, i, hh: (b * nt + i, hh)),
        out_shape=jax.ShapeDtypeStruct((nb * t, MLA_HEADS * MLA_QK_PAD), BF16),
        compiler_params=_cparams("parallel", "parallel", "arbitrary"),
        name="mla_q_proj",
    )(h, g, w, *tabs)


def _kv_body(x_ref, g_ref, wk_ref, wv_ref, kpe_ref, c_ref, k_ref, v_ref):
    cn = _rms(x_ref[...]) * g_ref[...]

    @pl.when(pl.program_id(1) == 0)
    def _():
        c_ref[...] = cn

    k_ref[:, :LANE] = _dot(cn, wk_ref[...]).astype(k_ref.dtype)
    k_ref[:, LANE:] = kpe_ref[...].astype(k_ref.dtype)
    v_ref[...] = _dot(cn, wv_ref[...]).astype(v_ref.dtype)


def _kv_proj(h, col_blk, g, wk, wv, kpe_pad, tm):
    m = h.shape[0]
    tm = min(tm, m)
    r = MLA_KV_RANK
    return pl.pallas_call(
        _kv_body,
        grid=(m // tm, MLA_HEADS),
        in_specs=[pl.BlockSpec((tm, r), lambda i, hh: (i, col_blk)),
                  pl.BlockSpec((1, r), lambda i, hh: (0, 0)),
                  pl.BlockSpec((r, MLA_NOPE), lambda i, hh: (0, hh)),
                  pl.BlockSpec((r, MLA_VD), lambda i, hh: (0, hh)),
                  pl.BlockSpec((tm, LANE), lambda i, hh: (i, 0))],
        out_specs=[pl.BlockSpec((tm, r), lambda i, hh: (i, 0)),
                   pl.BlockSpec((tm, MLA_QK_PAD), lambda i, hh: (i, hh)),
                   pl.BlockSpec((tm, MLA_VD), lambda i, hh: (i, hh))],
        out_shape=[jax.ShapeDtypeStruct((m, r), F32),
                   jax.ShapeDtypeStruct((m, MLA_HEADS * MLA_QK_PAD), BF16),
                   jax.ShapeDtypeStruct((m, MLA_V_W), BF16)],
        compiler_params=_cparams("parallel", "arbitrary"),
        name="mla_kv_proj",
    )(h, g, wk, wv, kpe_pad)


def _norm_body(x_ref, g_ref, o_ref):
    o_ref[...] = _rms(x_ref[...]) * g_ref[...]


def _rms_norm_cols(h, col_blk, width, g):
    m = h.shape[0]
    return pl.pallas_call(
        _norm_body,
        grid=(1,),
        in_specs=[pl.BlockSpec((m, width), lambda i: (0, col_blk)),
                  pl.BlockSpec((1, width), lambda i: (0, 0))],
        out_specs=pl.BlockSpec((m, width), lambda i: (0, 0)),
        out_shape=jax.ShapeDtypeStruct((m, width), F32),
        compiler_params=_cparams("arbitrary"),
        name="rms_norm",
    )(h, g)


def _flash_body(qi_ref, ki_ref, q_ref, k_ref, v_ref, g_ref, o_ref, m_s, l_s, acc_s, *, blk):
    p_id = pl.program_id(2)
    qi = qi_ref[p_id]
    ki = ki_ref[p_id]

    @pl.when(ki == 0)
    def _():
        m_s[...] = jnp.full(m_s.shape, -jnp.inf, F32)
        l_s[...] = jnp.zeros(l_s.shape, F32)
        acc_s[...] = jnp.zeros(acc_s.shape, F32)

    s = _dot_nt(q_ref[...], k_ref[...]) * MLA_SCALE
    row = lax.broadcasted_iota(jnp.int32, (blk, blk), 0) + qi * blk
    col = lax.broadcasted_iota(jnp.int32, (blk, blk), 1) + ki * blk
    s = jnp.where(col <= row, s, -jnp.inf)
    m_old = m_s[...]
    m_new = jnp.maximum(m_old, jnp.max(s, axis=-1, keepdims=True))
    alpha = jnp.exp(m_old - m_new)
    p = jnp.exp(s - m_new)
    l_s[...] = alpha * l_s[...] + jnp.sum(p, axis=-1, keepdims=True)
    acc_s[...] = alpha * acc_s[...] + _dot(p, v_ref[...])
    m_s[...] = m_new

    @pl.when(ki == qi)
    def _():
        o_ref[...] = (acc_s[...] / l_s[...] * _silu(g_ref[...])).astype(o_ref.dtype)


def _flash_mla(qf, kf, v, h, gate_blk, nb, t, blk):
    blk = min(blk, t)
    nq = t // blk
    pairs = [(a, b) for a in range(nq) for b in range(a + 1)]
    qi_tab = jnp.asarray(np.array([p[0] for p in pairs], np.int32))
    ki_tab = jnp.asarray(np.array([p[1] for p in pairs], np.int32))
    grid_spec = pltpu.PrefetchScalarGridSpec(
        num_scalar_prefetch=2,
        grid=(nb, MLA_HEADS, len(pairs)),
        in_specs=[
            pl.BlockSpec((blk, MLA_QK_PAD), lambda b, hh, p, qt, kt: (b * nq + qt[p], hh)),
            pl.BlockSpec((blk, MLA_QK_PAD), lambda b, hh, p, qt, kt: (b * nq + kt[p], hh)),
            pl.BlockSpec((blk, MLA_VD), lambda b, hh, p, qt, kt: (b * nq + kt[p], hh)),
            pl.BlockSpec((blk, MLA_VD), lambda b, hh, p, qt, kt: (b * nq + qt[p], gate_blk + hh)),
        ],
        out_specs=pl.BlockSpec((blk, MLA_VD), lambda b, hh, p, qt, kt: (b * nq + qt[p], hh)),
        scratch_shapes=[pltpu.VMEM((blk, 1), F32), pltpu.VMEM((blk, 1), F32),
                        pltpu.VMEM((blk, MLA_VD), F32)],
    )
    return pl.pallas_call(
        functools.partial(_flash_body, blk=blk),
        grid_spec=grid_spec,
        out_shape=jax.ShapeDtypeStruct((nb * t, MLA_V_W), BF16),
        compiler_params=_cparams("parallel", "parallel", "arbitrary"),
        name="mla_flash",
    )(qi_tab, ki_tab, qf, kf, v, h)


def _ret_body(lg_ref, q_ref, k_ref, v_ref, g_ref, cos_ref, sin_ref, o_ref, so_ref, s_scr):
    hh = pl.program_id(1)
    t = pl.program_id(2)
    c = CHUNK_RET

    @pl.when(t == 0)
    def _():
        s_scr[...] = jnp.zeros(s_scr.shape, F32)

    lg = lg_ref[hh]
    cos = cos_ref[...]
    sin = sin_ref[...]
    q = _rope_half(q_ref[...], cos, sin)
    k = _rope_half(k_ref[...], cos, sin) * (RET_DK ** -0.5)
    v = v_ref[...]
    rows = lax.broadcasted_iota(jnp.int32, (c, 1), 0).astype(F32)
    q_dec = jnp.exp((rows + 1.0) * lg)
    k_dec = jnp.exp((c - 1.0 - rows) * lg)
    jj = lax.broadcasted_iota(jnp.int32, (c, c), 0)
    ll = lax.broadcasted_iota(jnp.int32, (c, c), 1)
    decay = jnp.where(ll <= jj, jnp.exp((jj - ll).astype(F32) * lg), 0.0)
    s_old = s_scr[...]
    inter = _dot(q, s_old) * q_dec
    scores = _dot_nt(q, k) * decay
    out = inter + _dot(scores, v)
    s_new = jnp.exp(jnp.full((1, RET_DV), c * lg, F32)) * s_old + _dot_tn(k * k_dec, v)
    s_scr[...] = s_new
    o_ref[...] = (_rms(out) * _silu(g_ref[...])).astype(o_ref.dtype)

    @pl.when(t == pl.num_programs(2) - 1)
    def _():
        so_ref[...] = s_new


def _retention_prompt(h, lg, cos, sin, nb, t):
    c = CHUNK_RET
    nt = t // c
    nh = RET_HEADS

    def col(off):
        return pl.BlockSpec((c, RET_DK), lambda b, hh, i: (b * nt + i, off // RET_DK + hh))

    tab = pl.BlockSpec((c, LANE), lambda b, hh, i: (i, 0))
    return pl.pallas_call(
        _ret_body,
        grid=(nb, nh, nt),
        in_specs=[pl.BlockSpec(memory_space=pltpu.SMEM),
                  col(E_RQ), col(E_RK), col(E_RV), col(E_RG), tab, tab],
        out_specs=[pl.BlockSpec((c, RET_DV), lambda b, hh, i: (b * nt + i, hh)),
                   pl.BlockSpec((None, None, RET_DK, RET_DV), lambda b, hh, i: (b, hh, 0, 0))],
        out_shape=[jax.ShapeDtypeStruct((nb * t, RET_W), BF16),
                   jax.ShapeDtypeStruct((nb, nh, RET_DK, RET_DV), F32)],
        scratch_shapes=[pltpu.VMEM((RET_DK, RET_DV), F32)],
        compiler_params=_cparams("parallel", "parallel", "arbitrary"),
        name="retention_prompt",
    )(lg, h, h, h, h, cos, sin)


def _ret_step_body(lg_ref, q_ref, k_ref, v_ref, g_ref, cos_ref, sin_ref, s_ref, o_ref, so_ref):
    j = pl.program_id(1)
    cos = cos_ref[...]
    sin = sin_ref[...]
    for hh in range(RET_HEADS):
        sl = slice(hh * RET_DK, (hh + 1) * RET_DK)
        q = _rope_half(q_ref[pl.ds(j, 1), sl], cos, sin)
        k = _rope_half(k_ref[pl.ds(j, 1), sl], cos, sin) * (RET_DK ** -0.5)
        v = v_ref[pl.ds(j, 1), sl]
        g = g_ref[pl.ds(j, 1), sl]
        gamma = jnp.exp(jnp.full((1, RET_DV), lg_ref[hh], F32))
        s_new = gamma * s_ref[hh] + _row_to_col(k) * v
        so_ref[hh] = s_new
        out = jnp.sum(_row_to_col(q) * s_new, axis=0, keepdims=True)
        o_ref[pl.ds(j, 1), sl] = (_rms(out) * _silu(g)).astype(o_ref.dtype)


def _retention_step(h, lg, cos, sin, state, layer_idx):
    db = h.shape[0]
    rb = 8
    nh = RET_HEADS

    def col(off):
        return pl.BlockSpec((rb, RET_W), lambda i, j: (i, off // RET_W))

    tab = pl.BlockSpec((1, LANE), lambda i, j: (0, 0))
    return pl.pallas_call(
        _ret_step_body,
        grid=(db // rb, rb),
        in_specs=[pl.BlockSpec(memory_space=pltpu.SMEM),
                  col(E_RQ), col(E_RK), col(E_RV), col(E_RG), tab, tab,
                  pl.BlockSpec((None, None, nh, RET_DK, RET_DV),
                               lambda i, j: (layer_idx, i * rb + j, 0, 0, 0))],
        out_specs=[pl.BlockSpec((rb, RET_W), lambda i, j: (i, 0)),
                   pl.BlockSpec((None, nh, RET_DK, RET_DV), lambda i, j: (i * rb + j, 0, 0, 0))],
        out_shape=[jax.ShapeDtypeStruct((db, RET_W), F32),
                   jax.ShapeDtypeStruct((db, nh, RET_DK, RET_DV), F32)],
        compiler_params=_cparams("parallel", "arbitrary"),
        name="retention_step",
    )(lg, h, h, h, h, cos, sin, state)


def _gate_body(x_ref, w_ref, b_ref, o_ref):
    z = _dot(x_ref[...], w_ref[...]) + b_ref[...]
    o_ref[...] = (jnp.minimum(z, 0.0) - jnp.log1p(jnp.exp(-jnp.abs(z)))) / GLA_TAU


def _gla_gate(glr_pad, w2_pad, b, tm):
    m = glr_pad.shape[0]
    tm = min(tm, m)
    return pl.pallas_call(
        _gate_body,
        grid=(m // tm,),
        in_specs=[pl.BlockSpec((tm, LANE), lambda i: (i, 0)),
                  pl.BlockSpec((LANE, GLA_K_W), lambda i: (0, 0)),
                  pl.BlockSpec((1, GLA_K_W), lambda i: (0, 0))],
        out_specs=pl.BlockSpec((tm, GLA_K_W), lambda i: (i, 0)),
        out_shape=jax.ShapeDtypeStruct((m, GLA_K_W), F32),
        compiler_params=_cparams("parallel"),
        name="gla_gate",
    )(glr_pad, w2_pad, b)


def _gla_body(q_ref, k_ref, v_ref, g_ref, la_ref, ng_ref, o_ref, so_ref, st_scr):
    t = pl.program_id(2)
    c = CHUNK_GLA

    @pl.when(t == 0)
    def _():
        st_scr[...] = jnp.zeros(st_scr.shape, F32)

    q = q_ref[...] * (GLA_DK ** -0.5)
    k = k_ref[...]
    v = v_ref[...]
    la = la_ref[...]
    rows = lax.broadcasted_iota(jnp.int32, (c, GLA_DK), 0)
    b = la
    sh = 1
    while sh < c:
        b = b + jnp.where(rows >= sh, pltpu.roll(b, sh, 0), 0.0)
        sh *= 2
    b_last = b[c - 1:c, :]
    st = st_scr[...]
    inter = _dot_nt(q * jnp.exp(b), st)

    lanes = lax.broadcasted_iota(jnp.int32, (c, c), 1)
    a = jnp.zeros((c, c), F32)
    for l in range(c):
        w = jnp.exp(jnp.minimum(b - b[l:l + 1, :], 0.0))
        colv = jnp.sum(q * w * k[l:l + 1, :], axis=1, keepdims=True)
        a = jnp.where(lanes == l, colv, a)
    jrow = lax.broadcasted_iota(jnp.int32, (c, c), 0)
    a = jnp.where(lanes <= jrow, a, 0.0)
    out = inter + _dot(a, v)
    st_new = jnp.exp(b_last) * st + _dot_tn(v, k * jnp.exp(b_last - b))
    st_scr[...] = st_new
    o_ref[...] = (_rms(out) * ng_ref[...] * _silu(g_ref[...])).astype(o_ref.dtype)

    @pl.when(t == pl.num_programs(2) - 1)
    def _():
        so_ref[...] = st_new.T


def _gla_prompt(h, log_a, norm_g, nb, t):
    c = CHUNK_GLA
    nt = t // c
    nh = GLA_HEADS
    return pl.pallas_call(
        _gla_body,
        grid=(nb, nh, nt),
        in_specs=[pl.BlockSpec((c, GLA_DK), lambda b, hh, i: (b * nt + i, O_GQ // GLA_DK + hh)),
                  pl.BlockSpec((c, GLA_DK), lambda b, hh, i: (b * nt + i, O_GK // GLA_DK + hh)),
                  pl.BlockSpec((c, GLA_DV), lambda b, hh, i: (b * nt + i, O_GV // GLA_DV + hh)),
                  pl.BlockSpec((c, GLA_DV), lambda b, hh, i: (b * nt + i, O_GG // GLA_DV + hh)),
                  pl.BlockSpec((c, GLA_DK), lambda b, hh, i: (b * nt + i, hh)),
                  pl.BlockSpec((1, GLA_DV), lambda b, hh, i: (0, 0))],
        out_specs=[pl.BlockSpec((c, GLA_DV), lambda b, hh, i: (b * nt + i, hh)),
                   pl.BlockSpec((None, None, GLA_DK, GLA_DV), lambda b, hh, i: (b, hh, 0, 0))],
        out_shape=[jax.ShapeDtypeStruct((nb * t, GLA_V_W), BF16),
                   jax.ShapeDtypeStruct((nb, nh, GLA_DK, GLA_DV), F32)],
        scratch_shapes=[pltpu.VMEM((GLA_DV, GLA_DK), F32)],
        compiler_params=_cparams("parallel", "parallel", "arbitrary"),
        name="gla_prompt",
    )(h, h, h, h, log_a, norm_g)


def _gla_step_body(q_ref, k_ref, v_ref, g_ref, la_ref, ng_ref, s_ref, o_ref, so_ref):
    j = pl.program_id(1)
    for hh in range(GLA_HEADS):
        ks = slice(hh * GLA_DK, (hh + 1) * GLA_DK)
        vs = slice(hh * GLA_DV, (hh + 1) * GLA_DV)
        q = q_ref[pl.ds(j, 1), ks] * (GLA_DK ** -0.5)
        k = k_ref[pl.ds(j, 1), ks]
        v = v_ref[pl.ds(j, 1), vs]
        g = g_ref[pl.ds(j, 1), vs]
        dec = jnp.exp(la_ref[pl.ds(j, 1), ks])
        s_new = _row_to_col(dec) * s_ref[hh] + _row_to_col(k) * v
        so_ref[hh] = s_new
        out = jnp.sum(_row_to_col(q) * s_new, axis=0, keepdims=True)
        o_ref[pl.ds(j, 1), vs] = (_rms(out) * ng_ref[...] * _silu(g)).astype(o_ref.dtype)


def _gla_step(h, log_a, norm_g, state, layer_idx):
    db = h.shape[0]
    rb = 8
    nh = GLA_HEADS
    return pl.pallas_call(
        _gla_step_body,
        grid=(db // rb, rb),
        in_specs=[pl.BlockSpec((rb, GLA_K_W), lambda i, j: (i, O_GQ // GLA_K_W)),
                  pl.BlockSpec((rb, GLA_K_W), lambda i, j: (i, O_GK // GLA_K_W)),
                  pl.BlockSpec((rb, GLA_V_W), lambda i, j: (i, O_GV // GLA_V_W)),
                  pl.BlockSpec((rb, GLA_V_W), lambda i, j: (i, O_GG // GLA_V_W)),
                  pl.BlockSpec((rb, GLA_K_W), lambda i, j: (i, 0)),
                  pl.BlockSpec((1, GLA_DV), lambda i, j: (0, 0)),
                  pl.BlockSpec((None, None, nh, GLA_DK, GLA_DV),
                               lambda i, j: (layer_idx, i * rb + j, 0, 0, 0))],
        out_specs=[pl.BlockSpec((rb, GLA_V_W), lambda i, j: (i, 0)),
                   pl.BlockSpec((None, nh, GLA_DK, GLA_DV), lambda i, j: (i * rb + j, 0, 0, 0))],
        out_shape=[jax.ShapeDtypeStruct((db, GLA_V_W), F32),
                   jax.ShapeDtypeStruct((db, nh, GLA_DK, GLA_DV), F32)],
        compiler_params=_cparams("parallel", "arbitrary"),
        name="gla_step",
    )(h, h, h, h, log_a, norm_g, state)


def _mem_heads(q, g, mk_of, mv_of):
    outs = []
    for hh in range(MEM_HEADS):
        sl = slice(hh * MEM_HD, (hh + 1) * MEM_HD)
        s = _dot_nt(q[:, sl], mk_of(hh)) * (MEM_HD ** -0.5)
        s = s - jnp.max(s, axis=-1, keepdims=True)
        p = jnp.exp(s)
        p = p / jnp.sum(p, axis=-1, keepdims=True)
        outs.append(_dot(p, mv_of(hh)))
    return jnp.concatenate(outs, axis=-1) * _silu(g)


def _mem_prompt_body(q_ref, g_ref, mk_ref, mv_ref, o_ref):
    o_ref[...] = _mem_heads(
        q_ref[...], g_ref[...],
        lambda hh: mk_ref[:, hh * MEM_HD:(hh + 1) * MEM_HD],
        lambda hh: mv_ref[:, hh * MEM_HD:(hh + 1) * MEM_HD]).astype(o_ref.dtype)


def _mem_attn_prompt(h, q_blk, g_blk, mkv, layer, nb, t, tm):
    tm = min(tm, t)
    nt = t // tm
    return pl.pallas_call(
        _mem_prompt_body,
        grid=(nb, nt),
        in_specs=[pl.BlockSpec((tm, MEM_W), lambda b, i: (b * nt + i, q_blk)),
                  pl.BlockSpec((tm, MEM_W), lambda b, i: (b * nt + i, g_blk)),
                  pl.BlockSpec((MEM_LEN, MEM_W), lambda b, i: (b, layer)),
                  pl.BlockSpec((MEM_LEN, MEM_W), lambda b, i: (b, DEPTH + layer))],
        out_specs=pl.BlockSpec((tm, MEM_W), lambda b, i: (b * nt + i, 0)),
        out_shape=jax.ShapeDtypeStruct((nb * t, MEM_W), BF16),
        compiler_params=_cparams("parallel", "parallel"),
        name="mem_attn_prompt",
    )(h, h, mkv, mkv)


def _mem_step_body(q_ref, g_ref, mk_ref, mv_ref, o_ref):
    j = pl.program_id(1)
    q = jnp.broadcast_to(q_ref[pl.ds(j, 1), :], (8, MEM_W))
    g = g_ref[pl.ds(j, 1), :]
    out = _mem_heads(q, jnp.broadcast_to(g, (8, MEM_W)),
                     lambda hh: mk_ref[:, hh, :], lambda hh: mv_ref[:, hh, :])
    o_ref[pl.ds(j, 1), :] = out[0:1, :].astype(o_ref.dtype)


def _mem_attn_step(h, q_blk, g_blk, cache_k, cache_v, layer):
    db = h.shape[0]
    rb = 8
    cache = pl.BlockSpec((None, None, MEM_LEN, MEM_HEADS, MEM_HD),
                         lambda i, j: (layer, i * rb + j, 0, 0, 0))
    return pl.pallas_call(
        _mem_step_body,
        grid=(db // rb, rb),
        in_specs=[pl.BlockSpec((rb, MEM_W), lambda i, j: (i, q_blk)),
                  pl.BlockSpec((rb, MEM_W), lambda i, j: (i, g_blk)), cache, cache],
        out_specs=pl.BlockSpec((rb, MEM_W), lambda i, j: (i, 0)),
        out_shape=jax.ShapeDtypeStruct((db, MEM_W), F32),
        compiler_params=_cparams("parallel", "arbitrary"),
        name="mem_attn_step",
    )(h, h, cache_k, cache_v)


def _qlat_body(q_ref, w_ref, o_ref):
    o_ref[...] = _dot_nt(q_ref[...], w_ref[...]).astype(o_ref.dtype)


def _q_latent(qf, w_uk2d):
    db = qf.shape[0]
    r = MLA_KV_RANK
    return pl.pallas_call(
        _qlat_body,
        grid=(MLA_HEADS,),
        in_specs=[pl.BlockSpec((db, MLA_NOPE), lambda hh: (0, 2 * hh)),
                  pl.BlockSpec((r, MLA_NOPE), lambda hh: (0, hh))],
        out_specs=pl.BlockSpec((db, r), lambda hh: (0, hh)),
        out_shape=jax.ShapeDtypeStruct((db, MLA_HEADS * r), BF16),
        compiler_params=_cparams("parallel"),
        name="mla_q_latent",
    )(qf, w_uk2d)


def _paged_body(pt_ref, *refs, npg):
    ql_ref, qp_ref, cn_ref, kn_ref = refs[:4]
    ckv_refs = refs[4:4 + npg]
    kpe_refs = refs[4 + npg:4 + 2 * npg]
    o_ref = refs[4 + 2 * npg]
    m_s, l_s, acc_s = refs[5 + 2 * npg:]
    step = pl.program_id(1)

    @pl.when(step == 0)
    def _():
        m_s[...] = jnp.full(m_s.shape, -jnp.inf, F32)
        l_s[...] = jnp.zeros(l_s.shape, F32)
        acc_s[...] = jnp.zeros(acc_s.shape, F32)

    ql = ql_ref[...]
    qp = qp_ref[:, :MLA_ROPE]
    pages = [c[...].astype(BF16) for c in ckv_refs]
    s = [(_dot_nt(ql, pages[i]) + _dot_nt(qp, kpe_refs[i][...])) * MLA_SCALE for i in range(npg)]
    m_old = m_s[...]
    m_new = m_old
    for si in s:
        m_new = jnp.maximum(m_new, jnp.max(si, axis=-1, keepdims=True))
    alpha = jnp.exp(m_old - m_new)
    l_new = alpha * l_s[...]
    acc = alpha * acc_s[...]
    for i in range(npg):
        p = jnp.exp(s[i] - m_new)
        l_new = l_new + jnp.sum(p, axis=-1, keepdims=True)
        acc = acc + _dot(p, pages[i])
    m_s[...] = m_new
    l_s[...] = l_new
    acc_s[...] = acc

    @pl.when(step == pl.num_programs(1) - 1)
    def _():
        cn = cn_ref[...]
        kn = kn_ref[:, :MLA_ROPE]
        s_new = (jnp.sum(ql.astype(F32) * cn, axis=-1, keepdims=True)
                 + jnp.sum(qp.astype(F32) * kn, axis=-1, keepdims=True)) * MLA_SCALE
        m_fin = jnp.maximum(m_new, s_new)
        a2 = jnp.exp(m_new - m_fin)
        p_new = jnp.exp(s_new - m_fin)
        l_fin = a2 * l_new + p_new
        o = a2 * acc + p_new * cn
        o_ref[...] = (o / l_fin).astype(o_ref.dtype)


def _mla_paged(page_table, q_lat3, qf3, ckv_new3, kpe_new3, cache_ckv, cache_kpe, layer_idx, npg):
    db, n_pages = page_table.shape
    npg = min(npg, n_pages)
    r = MLA_KV_RANK
    in_specs = [
        pl.BlockSpec((None, MLA_HEADS, r), lambda b, s, pt: (b, 0, 0)),
        pl.BlockSpec((None, MLA_HEADS, LANE), lambda b, s, pt: (b, 0, 1)),
        pl.BlockSpec((None, 1, r), lambda b, s, pt: (b, 0, 0)),
        pl.BlockSpec((None, 1, LANE), lambda b, s, pt: (b, 0, 0)),
    ]
    for i in range(npg):
        in_specs.append(pl.BlockSpec((None, None, PAGE_SIZE, r),
                                     lambda b, s, pt, i=i: (layer_idx, pt[b, s * npg + i], 0, 0)))
    for i in range(npg):
        in_specs.append(pl.BlockSpec((None, None, PAGE_SIZE, MLA_ROPE),
                                     lambda b, s, pt, i=i: (layer_idx, pt[b, s * npg + i], 0, 0)))
    grid_spec = pltpu.PrefetchScalarGridSpec(
        num_scalar_prefetch=1,
        grid=(db, n_pages // npg),
        in_specs=in_specs,
        out_specs=pl.BlockSpec((None, MLA_HEADS, r), lambda b, s, pt: (b, 0, 0)),
        scratch_shapes=[pltpu.VMEM((MLA_HEADS, 1), F32), pltpu.VMEM((MLA_HEADS, 1), F32),
                        pltpu.VMEM((MLA_HEADS, r), F32)],
    )
    return pl.pallas_call(
        functools.partial(_paged_body, npg=npg),
        grid_spec=grid_spec,
        out_shape=jax.ShapeDtypeStruct((db, MLA_HEADS, r), BF16),
        compiler_params=_cparams("parallel", "arbitrary"),
        name="mla_paged",
    )(page_table, q_lat3, qf3, ckv_new3, kpe_new3, *([cache_ckv] * npg), *([cache_kpe] * npg))


def _olat_body(x_ref, w_ref, g_ref, o_ref):
    o_ref[...] = (_dot(x_ref[...], w_ref[...]) * _silu(g_ref[...])).astype(o_ref.dtype)


def _o_latent_proj(o_lat, w_uv2d, h, gate_blk):
    db = o_lat.shape[0]
    r = MLA_KV_RANK
    return pl.pallas_call(
        _olat_body,
        grid=(MLA_HEADS,),
        in_specs=[pl.BlockSpec((db, r), lambda hh: (0, hh)),
                  pl.BlockSpec((r, MLA_VD), lambda hh: (0, hh)),
                  pl.BlockSpec((db, MLA_VD), lambda hh: (0, gate_blk + hh))],
        out_specs=pl.BlockSpec((db, MLA_VD), lambda hh: (0, hh)),
        out_shape=jax.ShapeDtypeStruct((db, MLA_V_W), BF16),
        compiler_params=_cparams("parallel"),
        name="mla_o_latent_proj",
    )(o_lat, w_uv2d, h)


def _out_ln_body(a_ref, w_ref, x_ref, g_ref, b_ref, o_ref, ob_ref, acc_s):
    kk = pl.program_id(1)

    @pl.when(kk == 0)
    def _():
        acc_s[...] = jnp.zeros(acc_s.shape, F32)

    acc_s[...] += _dot(a_ref[...], w_ref[...])

    @pl.when(kk == pl.num_programs(1) - 1)
    def _():
        z = DN_ALPHA * x_ref[...] + acc_s[...]
        mu = jnp.mean(z, axis=-1, keepdims=True)
        zc = z - mu
        var = jnp.mean(zc * zc, axis=-1, keepdims=True)
        y = zc * lax.rsqrt(var + LN_EPS) * g_ref[...] + b_ref[...]
        o_ref[...] = y
        ob_ref[...] = y.astype(ob_ref.dtype)


def _out_proj_ln(a, w, x, g, b, tm, tk):
    m, k = a.shape
    d = w.shape[1]
    tm = min(tm, m)
    return pl.pallas_call(
        _out_ln_body,
        grid=(m // tm, k // tk),
        in_specs=[pl.BlockSpec((tm, tk), lambda i, kk: (i, kk)),
                  pl.BlockSpec((tk, d), lambda i, kk: (kk, 0)),
                  pl.BlockSpec((tm, d), lambda i, kk: (i, 0)),
                  pl.BlockSpec((1, d), lambda i, kk: (0, 0)),
                  pl.BlockSpec((1, d), lambda i, kk: (0, 0))],
        out_specs=[pl.BlockSpec((tm, d), lambda i, kk: (i, 0)),
                   pl.BlockSpec((tm, d), lambda i, kk: (i, 0))],
        out_shape=[jax.ShapeDtypeStruct((m, d), F32), jax.ShapeDtypeStruct((m, d), BF16)],
        scratch_shapes=[pltpu.VMEM((tm, d), F32)],
        compiler_params=_cparams("parallel", "arbitrary"),
        name="out_proj_ln",
    )(a, w, x, g, b)


def _rope_tables_half(pos):
    half = RET_DK // 2
    inv = ROPE_BASE ** (-jnp.arange(half, dtype=F32) / half)
    ang = pos.astype(F32)[:, None] * inv[None, :]
    return jnp.cos(ang), jnp.sin(ang)


def _rope_tables_small(pos):
    half = MLA_ROPE // 2
    inv = ROPE_BASE ** (-jnp.arange(half, dtype=F32) / half)
    ang = pos.astype(F32)[:, None] * inv[None, :]
    cos, sin = jnp.cos(ang), jnp.sin(ang)
    z = jnp.zeros_like(cos)
    cos_t = jnp.concatenate([cos, cos, z, z], axis=-1)
    sin_a = jnp.concatenate([-sin, z, z, z], axis=-1)
    sin_b = jnp.concatenate([z, sin, z, z], axis=-1)
    return cos_t, sin_a, sin_b


def _even_weights(w_in, w_uq, w_uk, w_uv):
    cut0 = E_CKV + MLA_KV_RANK
    w_main = jnp.concatenate([w_in[:, :cut0], w_in[:, cut0 + MLA_ROPE:]], axis=1).astype(BF16)
    w_kpe = jnp.pad(w_in[:, cut0:cut0 + MLA_ROPE], ((0, 0), (0, LANE - MLA_ROPE))).astype(BF16)
    wq = w_uq.reshape(MLA_Q_RANK, MLA_HEADS, MLA_NOPE + MLA_ROPE)
    wq = jnp.pad(wq, ((0, 0), (0, 0), (0, MLA_QK_PAD - MLA_NOPE - MLA_ROPE)))
    wq = wq.reshape(MLA_Q_RANK, MLA_HEADS * MLA_QK_PAD).astype(BF16)
    wk = w_uk.reshape(MLA_KV_RANK, MLA_HEADS * MLA_NOPE).astype(BF16)
    wv = w_uv.reshape(MLA_KV_RANK, MLA_HEADS * MLA_VD).astype(BF16)
    return w_main, w_kpe, wq, wk, wv


def _odd_weights(w_in, w_gk2):
    cut0 = O_GG + GLA_V_W
    w_main = jnp.concatenate([w_in[:, :cut0], w_in[:, cut0 + GLA_GATE_RANK:]], axis=1).astype(BF16)
    w_glr = jnp.pad(w_in[:, cut0:cut0 + GLA_GATE_RANK], ((0, 0), (0, LANE - GLA_GATE_RANK))).astype(BF16)
    w2 = jnp.pad(w_gk2, ((0, LANE - GLA_GATE_RANK), (0, 0))).astype(BF16)
    return w_main, w_glr, w2


def kernel(x_prompt, x_sample, mem_prompt, cache_mla_ckv, cache_mla_kpe, page_table, state_ret, state_gla,
           cache_mem_k, cache_mem_v, w_in_even, w_uq, w_uk, w_uv, q_norm_g, kv_norm_g, w_out_even,
           w_in_odd, w_gk2, b_gk, gla_norm_g, w_out_odd, w_mem_k, w_mem_v, ln_g, ln_b):
    nb, t, d = x_prompt.shape
    db = x_sample.shape[0]
    assert x_sample.shape[1] == 1, "sample group advances one token per step"
    n_pages = page_table.shape[1]
    past_len = n_pages * PAGE_SIZE
    mp = nb * t

    pos_p = jnp.arange(t)
    pos_s = jnp.full((db,), past_len, jnp.int32)
    half_p = _rope_tables_half(pos_p)
    half_s = tuple(a[:1] for a in _rope_tables_half(pos_s))
    small_p = _rope_tables_small(pos_p)
    small_s = _rope_tables_small(pos_s)
    lg = jnp.log1p(-jnp.exp2(-5.0 - jnp.arange(RET_HEADS, dtype=F32)))

    xp = x_prompt.reshape(mp, d)
    xs = x_sample.reshape(db, d)
    xp_b = xp.astype(BF16)
    xs_b = xs.astype(BF16)

    w_mem = jnp.concatenate([w_mem_k[l] for l in range(DEPTH)] + [w_mem_v[l] for l in range(DEPTH)],
                            axis=1).astype(BF16)
    mkv = _matmul(mem_prompt.reshape(nb * MEM_LEN, d).astype(BF16), w_mem, 512, 512)
    mkv4 = mkv.reshape(nb, MEM_LEN, 2 * DEPTH, MEM_HEADS, MEM_HD)
    memk_p = jnp.moveaxis(mkv4[:, :, :DEPTH], 2, 0)
    memv_p = jnp.moveaxis(mkv4[:, :, DEPTH:], 2, 0)

    ret_p, ret_s, gla_p, gla_s = [], [], [], []
    ckv_p, kpe_p, ckv_s, kpe_s = [], [], [], []
    for layer in range(DEPTH):
        g_ln = ln_g[layer].reshape(1, d)
        b_ln = ln_b[layer].reshape(1, d)
        if layer % 2 == 0:
            e = layer // 2
            w_main, w_kpe, wq, wk, wv = _even_weights(w_in_even[e], w_uq[e], w_uk[e], w_uv[e])
            gq = q_norm_g[e].reshape(1, MLA_Q_RANK)
            gkv = kv_norm_g[e].reshape(1, MLA_KV_RANK)
            w_out = w_out_even[e].astype(BF16)

            h = _matmul(xp_b, w_main, 1024, 512)
            kpe_pad = _kpe_proj(xp_b, w_kpe, small_p, nb, t, 1024)
            ro, s_ret = _retention_prompt(h, lg, half_p[0], half_p[1], nb, t)
            qf = _q_proj(h, E_CQ // MLA_Q_RANK, gq, wq, small_p, nb, t, 1024)
            cn, kf, vv = _kv_proj(h, E_CKV // MLA_KV_RANK, gkv, wk, wv, kpe_pad, 1024)
            mo = _flash_mla(qf, kf, vv, h, E_MG // MLA_VD, nb, t, 512)
            xo = _mem_attn_prompt(h, E_XQ // MEM_W, E_XG // MEM_W, mkv, layer, nb, t, 512)
            cat = jnp.concatenate([ro, mo, xo], axis=1)
            xp, xp_b = _out_proj_ln(cat, w_out, xp, g_ln, b_ln, 512, 512)
            ret_p.append(s_ret)
            ckv_p.append(cn.reshape(nb, t, MLA_KV_RANK))
            kpe_p.append(kpe_pad[:, :MLA_ROPE].reshape(nb, t, MLA_ROPE))

            hs = _matmul(xs_b, w_main, 128, 512)
            kpe_s_pad = _kpe_proj(xs_b, w_kpe, small_s, 1, db, 128)
            ro_s, s_ret_s = _retention_step(hs, lg, half_s[0], half_s[1], state_ret, e)
            qf_s = _q_proj(hs, E_CQ // MLA_Q_RANK, gq, wq, small_s, 1, db, 128)
            cn_s = _rms_norm_cols(hs, E_CKV // MLA_KV_RANK, MLA_KV_RANK, gkv)
            q_lat = _q_latent(qf_s, wk)
            o_lat = _mla_paged(page_table, q_lat.reshape(db, MLA_HEADS, MLA_KV_RANK),
                               qf_s.reshape(db, MLA_HEADS, MLA_QK_PAD),
                               cn_s.reshape(db, 1, MLA_KV_RANK), kpe_s_pad.reshape(db, 1, LANE),
                               cache_mla_ckv, cache_mla_kpe, e, 8)
            mo_s = _o_latent_proj(o_lat.reshape(db, MLA_HEADS * MLA_KV_RANK), wv, hs, E_MG // MLA_VD)
            xo_s = _mem_attn_step(hs, E_XQ // MEM_W, E_XG // MEM_W, cache_mem_k, cache_mem_v, layer)
            cat_s = jnp.concatenate([ro_s, mo_s, xo_s], axis=1)
            xs, xs_b = _out_proj_ln(cat_s, w_out, xs, g_ln, b_ln, 128, 512)
            ret_s.append(s_ret_s)
            ckv_s.append(cn_s.reshape(db, 1, MLA_KV_RANK))
            kpe_s.append(kpe_s_pad[:, :MLA_ROPE].reshape(db, 1, MLA_ROPE))
        else:
            o = layer // 2
            w_main, w_glr, w2 = _odd_weights(w_in_odd[o], w_gk2[o])
            bg = b_gk[o].reshape(1, GLA_K_W)
            ng = gla_norm_g[o].reshape(1, GLA_DV)
            w_out = w_out_odd[o].astype(BF16)

            h = _matmul(xp_b, w_main, 1024, 512)
            glr = _matmul(xp_b, w_glr, 1024, LANE)
            log_a = _gla_gate(glr, w2, bg, 1024)
            go, s_gla = _gla_prompt(h, log_a, ng, nb, t)
            xo = _mem_attn_prompt(h, O_XQ // MEM_W, O_XG // MEM_W, mkv, layer, nb, t, 512)
            cat = jnp.concatenate([go, xo], axis=1)
            xp, xp_b = _out_proj_ln(cat, w_out, xp, g_ln, b_ln, 512, 512)
            gla_p.append(s_gla)

            hs = _matmul(xs_b, w_main, 128, 512)
            glr_s = _matmul(xs_b, w_glr, 128, LANE)
            log_a_s = _gla_gate(glr_s, w2, bg, 128)
            go_s, s_gla_s = _gla_step(hs, log_a_s, ng, state_gla, o)
            xo_s = _mem_attn_step(hs, O_XQ // MEM_W, O_XG // MEM_W, cache_mem_k, cache_mem_v, layer)
            cat_s = jnp.concatenate([go_s, xo_s], axis=1)
            xs, xs_b = _out_proj_ln(cat_s, w_out, xs, g_ln, b_ln, 128, 512)
            gla_s.append(s_gla_s)

    return (xp.reshape(nb, t, d), xs.reshape(db, 1, d),
            jnp.stack(ret_p), jnp.stack(ret_s), jnp.stack(gla_p), jnp.stack(gla_s),
            jnp.stack(ckv_p), jnp.stack(kpe_p), jnp.stack(ckv_s), jnp.stack(kpe_s),
            memk_p, memv_p)
```
